```python
import jax, jax.numpy as jnp
from jax import lax
import numpy as np

D_MODEL = 1024
BATCH = 1
SEQ = 16384
DEPTH = 2
DEC_BATCH = 32
DEC_SEQ = 1
PAST_LEN = 16384
PAGE_SIZE = 128

N_A_LAYERS = DEPTH // 2
N_B_LAYERS = DEPTH - N_A_LAYERS
HEAD_DIM = 64
MEM_HEADS = 4
MEM_LEN = 256
MEM_W = MEM_HEADS * HEAD_DIM
TOK_W = D_MODEL - MEM_W
MIX_W = TOK_W + MEM_W
CHUNK = 128
SGU_GROUP_W = 128
SGU_GROUPS = TOK_W // SGU_GROUP_W
MOBA_HEADS = TOK_W // HEAD_DIM
MOBA_BLOCK = 256
MOBA_TOPK = 3
Q_BLOCK = 128
ROT_DIM = HEAD_DIM // 4
ROPE_THETA = 500000.0
N_GROUPS = 4
EXP_PER_GROUP = 8
N_EXPERTS = N_GROUPS * EXP_PER_GROUP
EXP_TOPK = 2
EXP_FF = D_MODEL // 2
EPS = 1e-6
F32 = jnp.float32

kernel_name = 'yoco_gmlp_moba_hmoe_decode_step'


def rms_norm(x, g):
    xf = x.astype(F32)
    y = xf * lax.rsqrt(jnp.mean(xf * xf, axis=-1, keepdims=True) + EPS)
    return y.astype(x.dtype) * g


def layer_norm(x, g, b):
    xf = x.astype(F32)
    mu = jnp.mean(xf, axis=-1, keepdims=True)
    var = jnp.mean(jnp.square(xf - mu), axis=-1, keepdims=True)
    return ((xf - mu) * lax.rsqrt(var + EPS)).astype(x.dtype) * g + b


def rope_partial(x, pos):
    half = ROT_DIM // 2
    inv = 1.0 / (ROPE_THETA ** (jnp.arange(half, dtype=F32) * 2.0 / ROT_DIM))
    ang = pos.astype(F32)[:, None] * inv[None, :]
    cos = jnp.cos(ang)[:, None, :].astype(x.dtype)
    sin = jnp.sin(ang)[:, None, :].astype(x.dtype)
    x1 = x[..., :half]
    x2 = x[..., half:ROT_DIM]
    return jnp.concatenate([x1 * cos - x2 * sin, x2 * cos + x1 * sin, x[..., ROT_DIM:]], axis=-1)


def mem_kv(mem, g_mem, w_mem_kv, g_mem_kn):
    B, M, _ = mem.shape
    kv = rms_norm(mem, g_mem) @ w_mem_kv
    k = rms_norm(kv[..., :MEM_W].reshape(B, M, MEM_HEADS, HEAD_DIM), g_mem_kn)
    v = kv[..., MEM_W:].reshape(B, M, MEM_HEADS, HEAD_DIM)
    return k, v


def mem_attend(q, mk, mv):
    s = jnp.einsum('bshd,bmhd->bhsm', q, mk).astype(F32) * (HEAD_DIM ** -0.5)
    p = jax.nn.softmax(s, axis=-1).astype(mv.dtype)
    return jnp.einsum('bhsm,bmhd->bshd', p, mv)


def shared_kv(h, pos, g_kv, w_kv, g_kn):
    B, S, _ = h.shape
    kv = rms_norm(h, g_kv) @ w_kv
    k = rope_partial(rms_norm(kv[..., :TOK_W].reshape(B, S, MOBA_HEADS, HEAD_DIM), g_kn), pos)
    v = kv[..., TOK_W:].reshape(B, S, MOBA_HEADS, HEAD_DIM)
    return k, v


def sgu_mix_chunks(vn, w_s, b_s):
    B, S, _ = vn.shape
    vg = vn.reshape(B, S // CHUNK, CHUNK, SGU_GROUPS, SGU_GROUP_W)
    out = jnp.einsum('gts,bnsgc->bntgc', jnp.tril(w_s), vg) + b_s.T[None, None, :, :, None]
    return out.reshape(B, S, TOK_W)


def sgu_mix_first_rows(vn, w_s, b_s):
    B, S, _ = vn.shape
    vg = vn.reshape(B, S, SGU_GROUPS, SGU_GROUP_W)
    out = jnp.einsum('gts,bsgc->btgc', jnp.tril(w_s[:, :S, :S]), vg) + b_s[:, :S].T[None, :, :, None]
    return out.reshape(B, S, TOK_W)


def moba_combine(s_sel, s_own, vs, vo):
    b, q, h, J, L = s_sel.shape
    p = jax.nn.softmax(jnp.concatenate([s_sel.reshape(b, q, h, J * L), s_own], axis=-1), axis=-1)
    p_sel = p[..., :J * L].reshape(b, q, h, J, L).astype(vs.dtype)
    p_own = p[..., J * L:].astype(vo.dtype)
    return (jnp.einsum('bqhjl,bqhjld->bqhd', p_sel, vs)
            + jnp.einsum('bqhl,blhd->bqhd', p_own, vo))


def pad_block_scores(s_blk, n_blocks):
    if n_blocks < MOBA_TOPK:
        fill = jnp.full(s_blk.shape[:-1] + (MOBA_TOPK - n_blocks,), -jnp.inf, F32)
        s_blk = jnp.concatenate([s_blk, fill], axis=-1)
    return s_blk


def moba_prompt(q, k, v):
    B, S, H, Dh = q.shape
    nb = -(-S // MOBA_BLOCK)
    pad = nb * MOBA_BLOCK - S
    kp = jnp.pad(k, ((0, 0), (0, pad), (0, 0), (0, 0)))
    vp = jnp.pad(v, ((0, 0), (0, pad), (0, 0), (0, 0)))
    means = kp.astype(F32).reshape(B, nb, MOBA_BLOCK, H, Dh).mean(axis=2)
    kbh = kp.reshape(B, nb, MOBA_BLOCK, H, Dh).transpose(0, 3, 1, 2, 4)
    vbh = vp.reshape(B, nb, MOBA_BLOCK, H, Dh).transpose(0, 3, 1, 2, 4)
    nqb = S // Q_BLOCK
    qb = q.reshape(B, nqb, Q_BLOCK, H, Dh).transpose(1, 0, 2, 3, 4)
    bi = jnp.arange(B)[:, None, None, None]
    hi = jnp.arange(H)[None, None, :, None]
    scale = Dh ** -0.5

    def step(args):
        qc, c = args
        start = c * Q_BLOCK
        blk = start // MOBA_BLOCK
        t = start + jnp.arange(Q_BLOCK)
        s_blk = jnp.einsum('bqhd,bnhd->bqhn', qc.astype(F32), means)
        s_blk = jnp.where(jnp.arange(nb) < blk, s_blk, -jnp.inf)
        s_blk = pad_block_scores(s_blk, nb)
        _, idx = lax.top_k(s_blk, MOBA_TOPK)
        valid = idx < blk
        idx = jnp.minimum(idx, nb - 1)
        ks = kbh[bi, hi, idx]
        vs = vbh[bi, hi, idx]
        s_sel = jnp.einsum('bqhd,bqhjld->bqhjl', qc, ks).astype(F32) * scale
        s_sel = jnp.where(valid[..., None], s_sel, -jnp.inf)
        ko = lax.dynamic_slice_in_dim(kp, blk * MOBA_BLOCK, MOBA_BLOCK, axis=1)
        vo = lax.dynamic_slice_in_dim(vp, blk * MOBA_BLOCK, MOBA_BLOCK, axis=1)
        s_own = jnp.einsum('bqhd,blhd->bqhl', qc, ko).astype(F32) * scale
        kpos = blk * MOBA_BLOCK + jnp.arange(MOBA_BLOCK)
        s_own = jnp.where((kpos[None, :] <= t[:, None])[None, :, None, :], s_own, -jnp.inf)
        return moba_combine(s_sel, s_own, vs, vo)

    out = lax.map(step, (qb, jnp.arange(nqb)))
    return out.transpose(1, 0, 2, 3, 4).reshape(B, S, H, Dh)


def moba_sample(q, k_new, v_new, pool_k, pool_v, page_table):
    B, S, H, Dh = q.shape
    ppb = MOBA_BLOCK // PAGE_SIZE
    n_pages = PAST_LEN // PAGE_SIZE
    blk = PAST_LEN // MOBA_BLOCK
    own_past = PAST_LEN - blk * MOBA_BLOCK
    scale = Dh ** -0.5
    if blk > 0:
        kpast = pool_k[page_table[:, :blk * ppb]]
        means = kpast.astype(F32).reshape(B, blk, MOBA_BLOCK, H, Dh).mean(axis=2)
        s_blk = jnp.einsum('bshd,bnhd->bshn', q.astype(F32), means)
    else:
        s_blk = jnp.zeros((B, S, H, 0), F32)
    s_blk = pad_block_scores(s_blk, blk)
    _, idx = lax.top_k(s_blk, MOBA_TOPK)
    valid = idx < blk
    lp = jnp.minimum(idx[..., None] * ppb + jnp.arange(ppb), n_pages - 1)
    phys = page_table[jnp.arange(B)[:, None, None, None, None], lp]
    hi = jnp.arange(H)[None, None, :, None, None, None]
    pg = jnp.arange(PAGE_SIZE)
    ks = pool_k[phys[..., None], pg, hi].reshape(B, S, H, MOBA_TOPK, MOBA_BLOCK, Dh)
    vs = pool_v[phys[..., None], pg, hi].reshape(B, S, H, MOBA_TOPK, MOBA_BLOCK, Dh)
    s_sel = jnp.einsum('bshd,bshjld->bshjl', q, ks).astype(F32) * scale
    s_sel = jnp.where(valid[..., None], s_sel, -jnp.inf)
    if own_past > 0:
        op = page_table[:, blk * ppb: blk * ppb + own_past // PAGE_SIZE]
        ko = jnp.concatenate([pool_k[op].reshape(B, own_past, H, Dh), k_new], axis=1)
        vo = jnp.concatenate([pool_v[op].reshape(B, own_past, H, Dh), v_new], axis=1)
    else:
        ko, vo = k_new, v_new
    col = jnp.arange(own_past + S)
    own_mask = (col[None, :] < own_past) | ((col[None, :] - own_past) <= jnp.arange(S)[:, None])
    s_own = jnp.einsum('bshd,blhd->bshl', q, ko).astype(F32) * scale
    s_own = jnp.where(own_mask[None, :, None, :], s_own, -jnp.inf)
    return moba_combine(s_sel, s_own, vs, vo)


def grouped_experts(xt, expert, gate, w_g, w_u, w_d):
    T, D = xt.shape
    A = T * EXP_TOPK
    bm = 128 if A >= 128 * N_EXPERTS else 8
    flat_e = expert.reshape(A)
    order = jnp.argsort(flat_e)
    sorted_e = flat_e[order]
    counts = jnp.bincount(flat_e, length=N_EXPERTS)
    starts = jnp.cumsum(counts) - counts
    padded = (counts + bm - 1) // bm * bm
    pend = jnp.cumsum(padded)
    pstarts = pend - padded
    dest_sorted = pstarts[sorted_e] + (jnp.arange(A) - starts[sorted_e])
    n_blocks = (A + N_EXPERTS * (bm - 1) + bm - 1) // bm
    slot_token = jnp.full((n_blocks * bm,), T, jnp.int32).at[dest_sorted].set((order // EXP_TOPK).astype(jnp.int32))
    block_expert = jnp.minimum(jnp.searchsorted(pend, jnp.arange(n_blocks) * bm, side='right'), N_EXPERTS - 1)
    x_pad = jnp.concatenate([xt, jnp.zeros((1, D), xt.dtype)], axis=0)

    def run_block(args):
        tok, e = args
        xb = x_pad[tok]
        hb = jax.nn.silu(xb @ w_g[e]) * (xb @ w_u[e])
        return hb @ w_d[e]

    out = lax.map(run_block, (slot_token.reshape(n_blocks, bm), block_expert)).reshape(n_blocks * bm, D)
    dest = jnp.zeros((A,), dest_sorted.dtype).at[order].set(dest_sorted)
    y = jnp.sum(out[dest].reshape(T, EXP_TOPK, D).astype(F32) * gate[..., None], axis=1)
    return y.astype(xt.dtype)


def hier_moe(x, w_r1, b_r1, w_r2, b_r2, w_g, w_u, w_d):
    B, S, D = x.shape
    xt = x.reshape(B * S, D)
    xf = xt.astype(F32)
    p1 = jax.nn.softmax(xf @ w_r1.astype(F32) + b_r1.astype(F32), axis=-1)
    pg, grp = lax.top_k(p1, 1)
    lg2 = (xf @ w_r2.astype(F32) + b_r2.astype(F32)).reshape(B * S, N_GROUPS, EXP_PER_GROUP)
    lg2 = jnp.take_along_axis(lg2, grp[:, :, None], axis=1)[:, 0]
    l2, e2 = lax.top_k(lg2, EXP_TOPK)
    gate = pg * jax.nn.softmax(l2, axis=-1)
    expert = grp * EXP_PER_GROUP + e2
    return grouped_experts(xt, expert, gate, w_g, w_u, w_d).reshape(B, S, D)


def trunk(x, pos, mem_k, mem_v, sgu_mix, moba_attend, w):
    B, S, _ = x.shape
    h = x
    chunk_v = []
    k_sh = None
    v_sh = None
    for l in range(DEPTH):
        xn = rms_norm(h, w['g_mix'][l])
        if l < N_A_LAYERS:
            z = xn @ w['w_in_a'][l]
            u = jax.nn.gelu(z[..., :TOK_W])
            vn = layer_norm(jax.nn.gelu(z[..., TOK_W:2 * TOK_W]), w['g_sgu'][l], w['b_sgu'][l])
            chunk_v.append(vn)
            tok = u * sgu_mix(vn, w['w_spatial'][l], w['b_spatial'][l])
            qm = z[..., 2 * TOK_W:]
        else:
            if k_sh is None:
                k_sh, v_sh = shared_kv(h, pos, w['g_kv'], w['w_kv'], w['g_kn'])
            j = l - N_A_LAYERS
            z = xn @ w['w_in_b'][j]
            q = rope_partial(rms_norm(z[..., :TOK_W].reshape(B, S, MOBA_HEADS, HEAD_DIM), w['g_qn'][j]), pos)
            tok = moba_attend(q, k_sh, v_sh).reshape(B, S, TOK_W)
            qm = z[..., TOK_W:]
        qm = rms_norm(qm.reshape(B, S, MEM_HEADS, HEAD_DIM), w['g_mem_qn'][l])
        mo = mem_attend(qm, mem_k[l], mem_v[l]).reshape(B, S, MEM_W)
        h = h + jnp.concatenate([tok, mo], axis=-1) @ w['w_out'][l]
        h = h + hier_moe(rms_norm(h, w['g_ffn'][l]), w['w_r1'][l], w['b_r1'][l], w['w_r2'][l], w['b_r2'][l],
                         w['w_e_gate'][l], w['w_e_up'][l], w['w_e_down'][l])
    return h, k_sh, v_sh, chunk_v


def setup_inputs(seed: int = 0) -> dict:
    key = jax.random.key(seed)
    ks = jax.random.split(key, 32)

    def nrm(k, shape, scale):
        return jax.random.normal(k, shape, F32) * scale

    def gain(k, shape):
        return 1.0 + 0.05 * jax.random.normal(k, shape, F32)

    n_pages = PAST_LEN // PAGE_SIZE
    n_pool = (DEC_BATCH * n_pages * 5 + 3) // 4
    page_table = jax.random.permutation(ks[7], n_pool)[:DEC_BATCH * n_pages].reshape(DEC_BATCH, n_pages).astype(jnp.int32)
    dsc = D_MODEL ** -0.5
    return {
        'x_prompt': nrm(ks[0], (BATCH, SEQ, D_MODEL), 1.0),
        'x_sample': nrm(ks[1], (DEC_BATCH, DEC_SEQ, D_MODEL), 1.0),
        'cache_k_pool': nrm(ks[2], (n_pool, PAGE_SIZE, MOBA_HEADS, HEAD_DIM), 1.0),
        'cache_v_pool': nrm(ks[3], (n_pool, PAGE_SIZE, MOBA_HEADS, HEAD_DIM), 1.0),
        'cache_mem_k': nrm(ks[4], (DEPTH, DEC_BATCH, MEM_LEN, MEM_HEADS, HEAD_DIM), 1.0),
        'cache_mem_v': nrm(ks[5], (DEPTH, DEC_BATCH, MEM_LEN, MEM_HEADS, HEAD_DIM), 1.0),
        'page_table': page_table,
        'mem_prompt': nrm(ks[6], (BATCH, MEM_LEN, D_MODEL), 1.0),
        'g_mix': gain(ks[8], (DEPTH, D_MODEL)),
        'g_ffn': gain(ks[9], (DEPTH, D_MODEL)),
        'w_out': nrm(ks[10], (DEPTH, MIX_W, D_MODEL), MIX_W ** -0.5),
        'g_mem': gain(ks[11], (DEPTH, D_MODEL)),
        'w_mem_kv': nrm(ks[12], (DEPTH, D_MODEL, 2 * MEM_W), dsc),
        'g_mem_qn': gain(ks[13], (DEPTH, HEAD_DIM)),
        'g_mem_kn': gain(ks[14], (DEPTH, HEAD_DIM)),
        'w_in_a': nrm(ks[15], (N_A_LAYERS, D_MODEL, 2 * TOK_W + MEM_W), dsc),
        'g_sgu': gain(ks[16], (N_A_LAYERS, TOK_W)),
        'b_sgu': nrm(ks[17], (N_A_LAYERS, TOK_W), 0.02),
        'w_spatial': nrm(ks[18], (N_A_LAYERS, SGU_GROUPS, CHUNK, CHUNK), CHUNK ** -0.5),
        'b_spatial': gain(ks[19], (N_A_LAYERS, SGU_GROUPS, CHUNK)),
        'g_kv': gain(ks[20], (D_MODEL,)),
        'w_kv': nrm(ks[21], (D_MODEL, 2 * TOK_W), dsc),
        'g_kn': gain(ks[22], (HEAD_DIM,)),
        'w_in_b': nrm(ks[23], (N_B_LAYERS, D_MODEL, TOK_W + MEM_W), dsc),
        'g_qn': gain(ks[24], (N_B_LAYERS, HEAD_DIM)),
        'w_r1': nrm(ks[25], (DEPTH, D_MODEL, N_GROUPS), dsc),
        'b_r1': nrm(ks[26], (DEPTH, N_GROUPS), 0.01),
        'w_r2': nrm(ks[27], (DEPTH, D_MODEL, N_EXPERTS), dsc),
        'b_r2': nrm(ks[28], (DEPTH, N_EXPERTS), 0.01),
        'w_e_gate': nrm(ks[29], (DEPTH, N_EXPERTS, D_MODEL, EXP_FF), dsc),
        'w_e_up': nrm(ks[30], (DEPTH, N_EXPERTS, D_MODEL, EXP_FF), dsc),
        'w_e_down': nrm(ks[31], (DEPTH, N_EXPERTS, EXP_FF, D_MODEL), EXP_FF ** -0.5),
    }


def reference(x_prompt, x_sample, cache_k_pool, cache_v_pool, cache_mem_k, cache_mem_v, page_table, mem_prompt,
              g_mix, g_ffn, w_out, g_mem, w_mem_kv, g_mem_qn, g_mem_kn, w_in_a, g_sgu, b_sgu, w_spatial, b_spatial,
              g_kv, w_kv, g_kn, w_in_b, g_qn, w_r1, b_r1, w_r2, b_r2, w_e_gate, w_e_up, w_e_down):
    w = {'g_mix': g_mix, 'g_ffn': g_ffn, 'w_out': w_out, 'g_mem_qn': g_mem_qn,
         'w_in_a': w_in_a, 'g_sgu': g_sgu, 'b_sgu': b_sgu, 'w_spatial': w_spatial, 'b_spatial': b_spatial,
         'g_kv': g_kv, 'w_kv': w_kv, 'g_kn': g_kn, 'w_in_b': w_in_b, 'g_qn': g_qn,
         'w_r1': w_r1, 'b_r1': b_r1, 'w_r2': w_r2, 'b_r2': b_r2,
         'w_e_gate': w_e_gate, 'w_e_up': w_e_up, 'w_e_down': w_e_down}
    mkv = [mem_kv(mem_prompt, g_mem[l], w_mem_kv[l], g_mem_kn[l]) for l in range(DEPTH)]
    mem_k_prompt = jnp.stack([m[0] for m in mkv])
    mem_v_prompt = jnp.stack([m[1] for m in mkv])
    pos_p = jnp.arange(x_prompt.shape[1])
    y_prompt, k_prompt, v_prompt, cv_p = trunk(x_prompt, pos_p, mem_k_prompt, mem_v_prompt,
                                               sgu_mix_chunks, moba_prompt, w)
    pos_s = PAST_LEN + jnp.arange(x_sample.shape[1])
    y_sample, k_sample, v_sample, cv_s = trunk(
        x_sample, pos_s, cache_mem_k, cache_mem_v, sgu_mix_first_rows,
        lambda q, k, v: moba_sample(q, k, v, cache_k_pool, cache_v_pool, page_table), w)
    chunk_v_prompt = jnp.stack([c[:, -CHUNK:] for c in cv_p])
    chunk_v_sample = jnp.stack(cv_s)
    return (y_prompt, y_sample, k_prompt, v_prompt, k_sample, v_sample, mem_k_prompt, mem_v_prompt, chunk_v_prompt, chunk_v_sample)
```

```python
import functools

import numpy as np
import jax
import jax.numpy as jnp
from jax import lax
from jax.experimental import pallas as pl
from jax.experimental.pallas import tpu as pltpu

F32 = jnp.float32
BF16 = jnp.bfloat16
I32 = jnp.int32

D_MODEL = 1024
HEAD_DIM = 64
MEM_HEADS = 4
MEM_LEN = 256
MEM_W = MEM_HEADS * HEAD_DIM
TOK_W = D_MODEL - MEM_W
IN_A_W = 2 * TOK_W + MEM_W
CHUNK = 128
SGU_GROUP_W = 128
SGU_GROUPS = TOK_W // SGU_GROUP_W
MOBA_HEADS = TOK_W // HEAD_DIM
MOBA_BLOCK = 256
MOBA_TOPK = 3
PAGE_SIZE = 128
PAGES_PER_BLOCK = MOBA_BLOCK // PAGE_SIZE
N_SEL_PAGES = MOBA_TOPK * PAGES_PER_BLOCK
ROT_DIM = HEAD_DIM // 4
ROPE_THETA = 500000.0
N_GROUPS = 4
EXP_PER_GROUP = 8
N_EXPERTS = N_GROUPS * EXP_PER_GROUP
EXP_TOPK = 2
EXP_FF = D_MODEL // 2
EPS = 1e-6
NEG = -1e30

LANES = 128
HEAD_BD = 256
TM = 256
MOE_BM = 256
MEANS_PAGES = 8
SAMPLE_TERMS = 3
VMEM_LIMIT = 48 * 1024 * 1024


def _cparams(n_axes, vmem=VMEM_LIMIT):
    return pltpu.CompilerParams(dimension_semantics=("arbitrary",) * n_axes, vmem_limit_bytes=vmem)


def _full(shape):
    zeros = (0,) * len(shape)
    return pl.BlockSpec(shape, lambda *_: zeros)


def _sds(shape, dtype=F32):
    return jax.ShapeDtypeStruct(shape, dtype)


def _dot(a, b):
    return jnp.dot(a, b, preferred_element_type=F32)


def _dot_nt(a, b):
    return lax.dot_general(a, b, (((1,), (1,)), ((), ())), preferred_element_type=F32)


def _mm_bf(a, w):
    return _dot(a.astype(BF16), w)


def _bf_round(x):
    return x.astype(BF16).astype(F32)


def _split(x, terms):
    parts = []
    for _ in range(terms):
        piece = x.astype(BF16)
        parts.append(piece)
        x = x - piece.astype(F32)
    return parts


def _rms(x, g):
    return x * lax.rsqrt(jnp.mean(x * x, axis=-1, keepdims=True) + EPS) * g


def _gelu(x):
    return 0.5 * x * (1.0 + jnp.tanh(0.7978845608028654 * (x + 0.044715 * (x * x * x))))


def _silu(x):
    return x / (1.0 + jnp.exp(-x))


def _head_sum(x, bd, terms):
    pieces = _split(x, terms)
    parts = []
    for c in range(x.shape[-1] // HEAD_BD):
        sl = slice(c * HEAD_BD, (c + 1) * HEAD_BD)
        acc = _dot(pieces[0][:, sl], bd)
        for piece in pieces[1:]:
            acc = acc + _dot(piece[:, sl], bd)
        parts.append(acc)
    return parts[0] if len(parts) == 1 else jnp.concatenate(parts, axis=-1)


def _head_rms(x, g, bd, terms=2):
    ms = _head_sum(x * x, bd, terms) * (1.0 / HEAD_DIM)
    return x * lax.rsqrt(ms + EPS) * g


def _rope_tables(pos, inv, sgn):
    ang = pos * inv
    return jnp.cos(ang), jnp.sin(ang) * sgn


def _rope(x, cos, sin):
    lane = lax.broadcasted_iota(I32, (1, LANES), 1) % HEAD_DIM
    first = lane < (ROT_DIM // 2)
    outs = []
    for c in range(x.shape[-1] // LANES):
        xc = x[:, c * LANES:(c + 1) * LANES]
        partner = jnp.where(first, pltpu.roll(xc, LANES - ROT_DIM // 2, 1), pltpu.roll(xc, ROT_DIM // 2, 1))
        outs.append(xc * cos + partner * sin)
    return jnp.concatenate(outs, axis=-1)


def _softmax_rows(s):
    m = jnp.max(s, axis=-1, keepdims=True)
    e = jnp.exp(s - m)
    return e / jnp.sum(e, axis=-1, keepdims=True)


def _mix_front(h, gmix, win, gsgu, bsgu):
    z = _mm_bf(_rms(h, gmix), win)
    u = _gelu(z[:, :TOK_W])
    v = _gelu(z[:, TOK_W:2 * TOK_W])
    mu = jnp.mean(v, axis=-1, keepdims=True)
    var = jnp.mean(jnp.square(v - mu), axis=-1, keepdims=True)
    vn = (v - mu) * lax.rsqrt(var + EPS) * gsgu + bsgu
    return u, vn, z[:, 2 * TOK_W:]


def _mem_attend_shared(qn, mkbd, mvbd):
    s = _mm_bf(qn, mkbd) * (HEAD_DIM ** -0.5)
    ps = [_softmax_rows(s[:, h * MEM_LEN:(h + 1) * MEM_LEN]) for h in range(MEM_HEADS)]
    return _mm_bf(jnp.concatenate(ps, axis=-1), mvbd)


def _router(lg):
    lane = lax.broadcasted_iota(I32, lg.shape, 1)
    lanef = lane.astype(F32)
    big = float(LANES)
    is1 = lane < N_GROUPS
    m1 = jnp.max(jnp.where(is1, lg, -jnp.inf), axis=-1, keepdims=True)
    e1 = jnp.where(is1, jnp.exp(jnp.where(is1, lg - m1, 0.0)), 0.0)
    p1 = e1 / jnp.sum(e1, axis=-1, keepdims=True)
    pg = jnp.max(p1, axis=-1, keepdims=True)
    grp = jnp.min(jnp.where(is1 & (p1 == pg), lanef, big), axis=-1, keepdims=True)
    lo = N_GROUPS + EXP_PER_GROUP * grp
    sel = (lanef >= lo) & (lanef < lo + EXP_PER_GROUP)
    v = jnp.where(sel, lg, -jnp.inf)
    v1 = jnp.max(v, axis=-1, keepdims=True)
    i1 = jnp.min(jnp.where(v == v1, lanef, big), axis=-1, keepdims=True)
    vv = jnp.where(lanef == i1, -jnp.inf, v)
    v2 = jnp.max(vv, axis=-1, keepdims=True)
    i2 = jnp.min(jnp.where(vv == v2, lanef, big), axis=-1, keepdims=True)
    t = jnp.exp(v2 - v1)
    g1 = pg / (1.0 + t)
    g2 = pg * t / (1.0 + t)
    eid = jnp.where(lane == 0, i1 - N_GROUPS, jnp.where(lane == 1, i2 - N_GROUPS, 0.0)).astype(I32)
    gate = jnp.where(lane == 0, g1, jnp.where(lane == 1, g2, 0.0))
    return eid, gate


def _out_router(h, tok, mo, wout_t, wout_m, gffn, wr, br):
    h1 = h + _mm_bf(tok, wout_t) + _mm_bf(mo, wout_m)
    xn2 = _rms(h1, gffn)
    eid, gate = _router(_mm_bf(xn2, wr) + br)
    return h1, xn2, eid, gate


def _moe_combine(h, y2, gate):
    return h + (y2[:, :D_MODEL] * gate[:, 0:1] + y2[:, D_MODEL:] * gate[:, 1:2])


def _kvq_core(h2, gkv, gmix, wkv, winb, gkn, gqn, gmq, bd, cos, sin, terms):
    hn = h2 * lax.rsqrt(jnp.mean(h2 * h2, axis=-1, keepdims=True) + EPS)
    kv = _mm_bf(hn * gkv, wkv)
    z = _mm_bf(hn * gmix, winb)
    k = _rope(_head_rms(kv[:, :TOK_W], gkn, bd, terms), cos, sin)
    v = kv[:, TOK_W:]
    q = _rope(_head_rms(z[:, :TOK_W], gqn, bd, terms), cos, sin)
    qn = _head_rms(z[:, TOK_W:], gmq, bd, terms)
    return k, v, q, qn


def _mem_kv_kernel(mem_ref, g_ref, w_ref, gkn_ref, bd_ref, kt_ref, vt_ref, mkbd_ref, mvbd_ref):
    kv = _mm_bf(_rms(mem_ref[...], g_ref[0]), w_ref[0])
    k = _head_rms(kv[:, :MEM_W], gkn_ref[0], bd_ref[...])
    v = kv[:, MEM_W:]
    kt = k.T
    kt_ref[0] = kt
    vt_ref[0] = v.T
    head_r = lax.broadcasted_iota(I32, (MEM_W, MEM_HEADS * MEM_LEN), 0) // HEAD_DIM
    blk_c = lax.broadcasted_iota(I32, (MEM_W, MEM_HEADS * MEM_LEN), 1) // MEM_LEN
    mkbd_ref[0] = jnp.where(head_r == blk_c, jnp.concatenate([kt] * MEM_HEADS, axis=1), 0.0).astype(BF16)
    blk_r = lax.broadcasted_iota(I32, (MEM_HEADS * MEM_LEN, MEM_W), 0) // MEM_LEN
    head_c = lax.broadcasted_iota(I32, (MEM_HEADS * MEM_LEN, MEM_W), 1) // HEAD_DIM
    mvbd_ref[0] = jnp.where(blk_r == head_c, jnp.concatenate([v] * MEM_HEADS, axis=0), 0.0).astype(BF16)


def _mem_kv(mem, g_mem, w_mem_kv_bf, gkn_t, bd):
    depth = g_mem.shape[0]
    per_layer = lambda r, c: pl.BlockSpec((1, r, c), lambda l: (l, 0, 0))
    wide = MEM_HEADS * MEM_LEN
    return pl.pallas_call(
        _mem_kv_kernel,
        grid=(depth,),
        in_specs=[_full((MEM_LEN, D_MODEL)), per_layer(1, D_MODEL), per_layer(D_MODEL, 2 * MEM_W), per_layer(1, MEM_W),
                  _full((HEAD_BD, HEAD_BD))],
        out_specs=[per_layer(MEM_W, MEM_LEN), per_layer(MEM_W, MEM_LEN), per_layer(MEM_W, wide), per_layer(wide, MEM_W)],
        out_shape=[_sds((depth, MEM_W, MEM_LEN)), _sds((depth, MEM_W, MEM_LEN)), _sds((depth, MEM_W, wide), BF16),
                   _sds((depth, wide, MEM_W), BF16)],
        compiler_params=_cparams(1),
        name="mem_kv",
    )(mem, g_mem.reshape(depth, 1, D_MODEL), w_mem_kv_bf, gkn_t.reshape(depth, 1, MEM_W), bd)


def _mix_a_prompt_kernel(h_ref, gmix_ref, win_ref, gsgu_ref, bsgu_ref, ws_ref, bs_ref, gqn_ref, bd_ref,
                         mkbd_ref, mvbd_ref, wout_t_ref, wout_m_ref, gffn_ref, wr_ref, br_ref,
                         h1_ref, vn_ref, xn2_ref, eid_ref, gate_ref, mix_ref):
    h = h_ref[...]
    u, vn, qm = _mix_front(h, gmix_ref[...], win_ref[...], gsgu_ref[...], bsgu_ref[...])
    row = lax.broadcasted_iota(I32, (CHUNK, CHUNK), 0)
    col = lax.broadcasted_iota(I32, (CHUNK, CHUNK), 1)
    for g in range(SGU_GROUPS):
        wg = jnp.where(row >= col, ws_ref[g], 0.0).astype(BF16)
        for c in range(TM // CHUNK):
            blk = vn[c * CHUNK:(c + 1) * CHUNK, g * SGU_GROUP_W:(g + 1) * SGU_GROUP_W].astype(BF16)
            mix_ref[c * CHUNK:(c + 1) * CHUNK, g * SGU_GROUP_W:(g + 1) * SGU_GROUP_W] = _dot(wg, blk) + bs_ref[g]
    tok = u * mix_ref[...]
    qn = _head_rms(qm, gqn_ref[...], bd_ref[...])
    mo = _mem_attend_shared(qn, mkbd_ref[...], mvbd_ref[...])
    h1, xn2, eid, gate = _out_router(h, tok, mo, wout_t_ref[...], wout_m_ref[...], gffn_ref[...],
                                     wr_ref[...], br_ref[...])
    h1_ref[...] = h1
    xn2_ref[...] = xn2
    eid_ref[...] = eid
    gate_ref[...] = gate

    @pl.when(pl.program_id(0) == pl.num_programs(0) - 1)
    def _():
        vn_ref[...] = vn[TM - CHUNK:, :]


def _mix_a_prompt(x, p):
    T = x.shape[0]
    tile = lambda w: pl.BlockSpec((TM, w), lambda i: (i, 0))
    return pl.pallas_call(
        _mix_a_prompt_kernel,
        grid=(T // TM,),
        in_specs=[
            tile(D_MODEL), _full((1, D_MODEL)), _full((D_MODEL, IN_A_W)), _full((1, TOK_W)), _full((1, TOK_W)),
            _full((SGU_GROUPS, CHUNK, CHUNK)), _full((SGU_GROUPS, CHUNK, SGU_GROUP_W)), _full((1, MEM_W)),
            _full((HEAD_BD, HEAD_BD)), _full((MEM_W, MEM_HEADS * MEM_LEN)), _full((MEM_HEADS * MEM_LEN, MEM_W)),
            _full((TOK_W, D_MODEL)), _full((MEM_W, D_MODEL)), _full((1, D_MODEL)),
            _full((D_MODEL, LANES)), _full((1, LANES)),
        ],
        out_specs=[tile(D_MODEL), _full((CHUNK, TOK_W)), tile(D_MODEL), tile(LANES), tile(LANES)],
        out_shape=[_sds((T, D_MODEL)), _sds((CHUNK, TOK_W)), _sds((T, D_MODEL)), _sds((T, LANES), I32),
                   _sds((T, LANES))],
        scratch_shapes=[pltpu.VMEM((TM, TOK_W), F32)],
        compiler_params=_cparams(1),
        name="mix_a_prompt",
    )(x, p["g_mix"][0], p["w_in_a"], p["g_sgu"], p["b_sgu"], p["w_spatial"], p["b_spatial_full"], p["g_mem_qn"][0],
      p["bd"], p["mkbd"][0], p["mvbd"][0], p["w_out_t"][0], p["w_out_m"][0], p["g_ffn"][0], p["wr"][0], p["br"][0])


def _mix_a_sample_kernel(h_ref, gmix_ref, win_ref, gsgu_ref, bsgu_ref, wd_ref, bdg_ref, gqn_ref, bd_ref,
                         tok_ref, vn_ref, qn_ref):
    u, vn, qm = _mix_front(h_ref[...], gmix_ref[...], win_ref[...], gsgu_ref[...], bsgu_ref[...])
    tok_ref[...] = u * (vn * wd_ref[...] + bdg_ref[...])
    vn_ref[...] = vn
    qn_ref[...] = _head_rms(qm, gqn_ref[...], bd_ref[...], SAMPLE_TERMS)


def _mix_a_sample(x, p):
    B = x.shape[0]
    return pl.pallas_call(
        _mix_a_sample_kernel,
        grid=(1,),
        in_specs=[
            _full((B, D_MODEL)), _full((1, D_MODEL)), _full((D_MODEL, IN_A_W)), _full((1, TOK_W)), _full((1, TOK_W)),
            _full((1, TOK_W)), _full((1, TOK_W)), _full((1, MEM_W)), _full((HEAD_BD, HEAD_BD)),
        ],
        out_specs=[_full((B, TOK_W)), _full((B, TOK_W)), _full((B, MEM_W))],
        out_shape=[_sds((B, TOK_W)), _sds((B, TOK_W)), _sds((B, MEM_W))],
        compiler_params=_cparams(1),
        name="mix_a_sample",
    )(x, p["g_mix"][0], p["w_in_a"], p["g_sgu"], p["b_sgu"], p["sgu_w00"], p["sgu_b0"], p["g_mem_qn"][0], p["bd"])


def _mem_attend_sample_kernel(q_ref, kt_ref, vt_ref, o_ref):
    prod = _bf_round(kt_ref[0, 0]) * _bf_round(q_ref[0])
    vt = _bf_round(vt_ref[0, 0])
    outs = []
    for h in range(MEM_HEADS):
        rows = slice(h * HEAD_DIM, (h + 1) * HEAD_DIM)
        s = jnp.sum(prod[rows], axis=0, keepdims=True) * (HEAD_DIM ** -0.5)
        outs.append(jnp.sum(vt[rows] * _bf_round(_softmax_rows(s)), axis=-1, keepdims=True))
    o_ref[0] = jnp.concatenate(outs, axis=0)


def _mem_attend_sample(qn, mem_kt, mem_vt, layer):
    B = qn.shape[0]
    kv_spec = pl.BlockSpec((1, 1, MEM_W, MEM_LEN), lambda b: (layer, b, 0, 0))
    col = pl.BlockSpec((1, MEM_W, 1), lambda b: (b, 0, 0))
    out = pl.pallas_call(
        _mem_attend_sample_kernel,
        grid=(B,),
        in_specs=[col, kv_spec, kv_spec],
        out_specs=col,
        out_shape=_sds((B, MEM_W, 1)),
        compiler_params=_cparams(1),
        name="mem_attend_sample",
    )(qn.reshape(B, MEM_W, 1), mem_kt, mem_vt)
    return out.reshape(B, MEM_W)


def _out_router_kernel(h_ref, tok_ref, mo_ref, wout_t_ref, wout_m_ref, gffn_ref, wr_ref, br_ref,
                       h1_ref, xn2_ref, eid_ref, gate_ref):
    h1, xn2, eid, gate = _out_router(h_ref[...], tok_ref[...], mo_ref[...], wout_t_ref[...], wout_m_ref[...],
                                     gffn_ref[...], wr_ref[...], br_ref[...])
    h1_ref[...] = h1
    xn2_ref[...] = xn2
    eid_ref[...] = eid
    gate_ref[...] = gate


def _out_router_prompt(h, tok, mo, p, layer):
    T = h.shape[0]
    tile = lambda w: pl.BlockSpec((TM, w), lambda i: (i, 0))
    return pl.pallas_call(
        _out_router_kernel,
        grid=(T // TM,),
        in_specs=[tile(D_MODEL), tile(TOK_W), tile(MEM_W), _full((TOK_W, D_MODEL)), _full((MEM_W, D_MODEL)),
                  _full((1, D_MODEL)), _full((D_MODEL, LANES)), _full((1, LANES))],
        out_specs=[tile(D_MODEL), tile(D_MODEL), tile(LANES), tile(LANES)],
        out_shape=[_sds((T, D_MODEL)), _sds((T, D_MODEL)), _sds((T, LANES), I32), _sds((T, LANES))],
        compiler_params=_cparams(1),
        name="out_router_prompt",
    )(h, tok, mo, p["w_out_t"][layer], p["w_out_m"][layer], p["g_ffn"][layer], p["wr"][layer], p["br"][layer])


def _out_router_sample_kernel(h_ref, tok_ref, mo_ref, wout_t_ref, wout_m_ref, gffn_ref, wr_ref, br_ref,
                              h1_ref, xn2_ref, gd_ref):
    h1, xn2, eid, gate = _out_router(h_ref[...], tok_ref[...], mo_ref[...], wout_t_ref[...], wout_m_ref[...],
                                     gffn_ref[...], wr_ref[...], br_ref[...])
    lane = lax.broadcasted_iota(I32, eid.shape, 1)
    gd = jnp.zeros(gate.shape, F32)
    for k in range(EXP_TOPK):
        gd = gd + jnp.where(lane == eid[:, k:k + 1], gate[:, k:k + 1], 0.0)
    h1_ref[...] = h1
    xn2_ref[...] = xn2
    gd_ref[...] = gd


def _out_router_sample(h, tok, mo, p, layer):
    B = h.shape[0]
    return pl.pallas_call(
        _out_router_sample_kernel,
        grid=(1,),
        in_specs=[_full((B, D_MODEL)), _full((B, TOK_W)), _full((B, MEM_W)), _full((TOK_W, D_MODEL)),
                  _full((MEM_W, D_MODEL)), _full((1, D_MODEL)), _full((D_MODEL, LANES)), _full((1, LANES))],
        out_specs=[_full((B, D_MODEL)), _full((B, D_MODEL)), _full((B, LANES))],
        out_shape=[_sds((B, D_MODEL)), _sds((B, D_MODEL)), _sds((B, LANES))],
        compiler_params=_cparams(1),
        name="out_router_sample",
    )(h, tok, mo, p["w_out_t"][layer], p["w_out_m"][layer], p["g_ffn"][layer], p["wr"][layer], p["br"][layer])


def _moe_sample_kernel(h_ref, x_ref, gd_ref, wg_ref, wu_ref, wd_ref, o_ref):
    e = pl.program_id(0)

    @pl.when(e == 0)
    def _():
        o_ref[...] = jnp.zeros(o_ref.shape, F32)

    lane = lax.broadcasted_iota(I32, gd_ref.shape, 1)
    g = jnp.sum(jnp.where(lane == e, gd_ref[...], 0.0), axis=-1, keepdims=True)
    x = x_ref[...]
    hb = _silu(_mm_bf(x, wg_ref[0].astype(BF16))) * _mm_bf(x, wu_ref[0].astype(BF16))
    o_ref[...] += jnp.where(g != 0.0, g * _mm_bf(hb, wd_ref[0].astype(BF16)), 0.0)

    @pl.when(e == pl.num_programs(0) - 1)
    def _():
        o_ref[...] = h_ref[...] + o_ref[...]


def _moe_sample(h, xn2, gd, w_g, w_u, w_d):
    B = h.shape[0]
    wspec = lambda s: pl.BlockSpec((1,) + s, lambda e: (e, 0, 0))
    return pl.pallas_call(
        _moe_sample_kernel,
        grid=(N_EXPERTS,),
        in_specs=[_full((B, D_MODEL)), _full((B, D_MODEL)), _full((B, LANES)),
                  wspec((D_MODEL, EXP_FF)), wspec((D_MODEL, EXP_FF)), wspec((EXP_FF, D_MODEL))],
        out_specs=_full((B, D_MODEL)),
        out_shape=_sds((B, D_MODEL)),
        compiler_params=_cparams(1),
        name="moe_sample",
    )(h, xn2, gd, w_g, w_u, w_d)


def _experts_kernel(be_ref, nval_ref, nact_ref, src_ref, dst_ref, x_hbm, wg_ref, wu_ref, wd_ref, y_hbm,
                    xbuf, ybuf, wgb, wub, wdb, gsem, ssem):
    i = pl.program_id(0)
    last = pl.num_programs(0) - 1
    nact = nact_ref[0]

    def gather_wait(slot):
        pltpu.make_async_copy(x_hbm.at[pl.ds(0, MOE_BM)], xbuf.at[slot], gsem.at[slot]).wait()

    def scatter_wait(jb):
        slot = jb % 2
        nv = nval_ref[jb]

        @pl.when(nv == MOE_BM)
        def _():
            pltpu.make_async_copy(ybuf.at[slot], y_hbm.at[pl.ds(0, MOE_BM)], ssem.at[slot]).wait()

        @pl.when(nv < MOE_BM)
        def _():
            def wait_row(r, carry):
                pltpu.make_async_copy(ybuf.at[slot, pl.ds(0, 1)], y_hbm.at[pl.ds(0, 1)], ssem.at[slot]).wait()
                return carry

            lax.fori_loop(0, nv, wait_row, 0)

    @pl.when(i < nact)
    def _():
        slot = i % 2

        def issue(r, carry):
            t = src_ref[0, 0, r]
            pltpu.make_async_copy(x_hbm.at[pl.ds(t, 1)], xbuf.at[slot, pl.ds(r, 1)], gsem.at[slot]).start()
            return carry

        lax.fori_loop(0, MOE_BM, issue, 0, unroll=8)

    j = i - 1

    @pl.when((j >= 0) & (j < nact))
    def _():
        slot = j % 2
        gather_wait(slot)

        @pl.when(j >= 2)
        def _():
            scatter_wait(j - 2)

        @pl.when((j == 0) | (be_ref[j] != be_ref[jnp.maximum(j - 1, 0)]))
        def _():
            wgb[...] = wg_ref[0].astype(BF16)
            wub[...] = wu_ref[0].astype(BF16)
            wdb[...] = wd_ref[0].astype(BF16)

        x = xbuf[slot]
        hb = _silu(_mm_bf(x, wgb[...])) * _mm_bf(x, wub[...])
        ybuf[slot] = _mm_bf(hb, wdb[...])

        def issue(r, carry):
            d = dst_ref[0, 0, r]
            pltpu.make_async_copy(ybuf.at[slot, pl.ds(r, 1)], y_hbm.at[pl.ds(d, 1)], ssem.at[slot]).start()
            return carry

        nv = nval_ref[j]

        @pl.when(nv == MOE_BM)
        def _():
            lax.fori_loop(0, MOE_BM, issue, 0, unroll=8)

        @pl.when(nv < MOE_BM)
        def _():
            lax.fori_loop(0, nv, issue, 0)

    @pl.when(i == last)
    def _():
        for back in (2, 1):
            jb = nact - back

            @pl.when(jb >= 0)
            def _():
                scatter_wait(jb)


def _moe_experts(xn2, eid, w_g, w_u, w_d):
    T = xn2.shape[0]
    A = T * EXP_TOPK
    bm = MOE_BM
    nb = (A + N_EXPERTS * (bm - 1) + bm - 1) // bm
    n_slots = nb * bm
    flat_e = eid.reshape(A)
    onehot = (flat_e[:, None] == jnp.arange(N_EXPERTS, dtype=I32)[None, :]).astype(I32)
    csum = jnp.cumsum(onehot, axis=0)
    rank = jnp.sum((csum - 1) * onehot, axis=1)
    counts = csum[-1]
    padded = (counts + bm - 1) // bm * bm
    pend = jnp.cumsum(padded)
    pstart = pend - padded
    slot_of = pstart[flat_e] + rank
    a_idx = jnp.arange(A, dtype=I32)
    src = jnp.zeros((n_slots,), I32).at[slot_of].set(a_idx // EXP_TOPK)
    dst = jnp.zeros((n_slots,), I32).at[slot_of].set(a_idx)
    nact = (pend[-1] // bm).astype(I32)
    blk = jnp.arange(nb, dtype=I32)
    be = jnp.minimum(jnp.searchsorted(pend, blk * bm, side="right"), N_EXPERTS - 1).astype(I32)
    nval = jnp.where(blk < nact, jnp.clip((pstart + counts)[be] - blk * bm, 0, bm), 0).astype(I32)
    be = jnp.where(blk < nact, be, be[jnp.maximum(nact - 1, 0)])

    wspec = lambda s: pl.BlockSpec((1,) + s, lambda i, be_r, *_: (be_r[jnp.maximum(i - 1, 0)], 0, 0))
    grid_spec = pltpu.PrefetchScalarGridSpec(
        num_scalar_prefetch=3,
        grid=(nb + 1,),
        in_specs=[
            pl.BlockSpec((1, 1, bm), lambda i, *_: (jnp.minimum(i, nb - 1), 0, 0), memory_space=pltpu.SMEM),
            pl.BlockSpec((1, 1, bm), lambda i, *_: (jnp.maximum(i - 1, 0), 0, 0), memory_space=pltpu.SMEM),
            pl.BlockSpec(memory_space=pl.ANY),
            wspec((D_MODEL, EXP_FF)), wspec((D_MODEL, EXP_FF)), wspec((EXP_FF, D_MODEL)),
        ],
        out_specs=pl.BlockSpec(memory_space=pl.ANY),
        scratch_shapes=[
            pltpu.VMEM((2, bm, D_MODEL), F32), pltpu.VMEM((2, bm, D_MODEL), F32),
            pltpu.VMEM((D_MODEL, EXP_FF), BF16), pltpu.VMEM((D_MODEL, EXP_FF), BF16), pltpu.VMEM((EXP_FF, D_MODEL), BF16),
            pltpu.SemaphoreType.DMA((2,)), pltpu.SemaphoreType.DMA((2,)),
        ],
    )
    y = pl.pallas_call(
        _experts_kernel,
        grid_spec=grid_spec,
        out_shape=_sds((A, D_MODEL)),
        compiler_params=_cparams(1),
        name="moe_experts",
    )(be, nval, nact.reshape(1), src.reshape(nb, 1, bm), dst.reshape(nb, 1, bm), xn2, w_g, w_u, w_d)
    return y.reshape(T, EXP_TOPK * D_MODEL)


def _kvq_prompt_kernel(h1_ref, y2_ref, gate_ref, gkv_ref, gmix_ref, wkv_ref, winb_ref, gkn_ref, gqn_ref, gmq_ref,
                       bd_ref, inv_ref, sgn_ref, mkbd_ref, mvbd_ref,
                       h2_ref, k_ref, v_ref, kb_ref, vb_ref, qb_ref, mean_ref, mo_ref):
    pos = (pl.program_id(0) * TM + lax.broadcasted_iota(I32, (TM, 1), 0)).astype(F32)
    cos, sin = _rope_tables(pos, inv_ref[...], sgn_ref[...])
    h2 = _moe_combine(h1_ref[...], y2_ref[...], gate_ref[...])
    k, v, q, qn = _kvq_core(h2, gkv_ref[...], gmix_ref[...], wkv_ref[...], winb_ref[...], gkn_ref[...], gqn_ref[...],
                            gmq_ref[...], bd_ref[...], cos, sin, 2)
    h2_ref[...] = h2
    kt = k.T
    vt = v.T
    k_ref[...] = kt
    v_ref[...] = vt
    kb_ref[0] = kt.astype(BF16)
    vb_ref[0] = vt.astype(BF16)
    qb_ref[...] = (q * (HEAD_DIM ** -0.5)).astype(BF16)
    mean_ref[0] = jnp.mean(k, axis=0, keepdims=True)
    mo_ref[...] = _mem_attend_shared(qn, mkbd_ref[...], mvbd_ref[...])


def _kvq_prompt(h1, y2, gate, p):
    T = h1.shape[0]
    nblk = T // TM
    tile = lambda w: pl.BlockSpec((TM, w), lambda i: (i, 0))
    cols = pl.BlockSpec((TOK_W, TM), lambda i: (0, i))
    blk3 = pl.BlockSpec((1, TOK_W, TM), lambda i: (i, 0, 0))
    return pl.pallas_call(
        _kvq_prompt_kernel,
        grid=(nblk,),
        in_specs=[
            tile(D_MODEL), tile(EXP_TOPK * D_MODEL), tile(LANES), _full((1, D_MODEL)), _full((1, D_MODEL)),
            _full((D_MODEL, 2 * TOK_W)), _full((D_MODEL, D_MODEL)), _full((1, TOK_W)), _full((1, TOK_W)),
            _full((1, MEM_W)), _full((HEAD_BD, HEAD_BD)), _full((1, LANES)), _full((1, LANES)),
            _full((MEM_W, MEM_HEADS * MEM_LEN)), _full((MEM_HEADS * MEM_LEN, MEM_W)),
        ],
        out_specs=[tile(D_MODEL), cols, cols, blk3, blk3, tile(TOK_W),
                   pl.BlockSpec((1, 1, TOK_W), lambda i: (i, 0, 0)), tile(MEM_W)],
        out_shape=[_sds((T, D_MODEL)), _sds((TOK_W, T)), _sds((TOK_W, T)), _sds((nblk, TOK_W, TM), BF16),
                   _sds((nblk, TOK_W, TM), BF16), _sds((T, TOK_W), BF16), _sds((nblk, 1, TOK_W)), _sds((T, MEM_W))],
        compiler_params=_cparams(1),
        name="kvq_prompt",
    )(h1, y2, gate, p["g_kv"], p["g_mix"][1], p["w_kv"], p["w_in_b"], p["g_kn"], p["g_qn"], p["g_mem_qn"][1],
      p["bd"], p["rope_inv"], p["rope_sgn"], p["mkbd"][1], p["mvbd"][1])


def _kvq_sample_kernel(pos_ref, h2_ref, gkv_ref, gmix_ref, wkv_ref, winb_ref, gkn_ref, gqn_ref, gmq_ref, bd_ref,
                       inv_ref, sgn_ref, k_ref, v_ref, q_ref, qn_ref):
    cos, sin = _rope_tables(pos_ref[...], inv_ref[...], sgn_ref[...])
    k, v, q, qn = _kvq_core(h2_ref[...], gkv_ref[...], gmix_ref[...], wkv_ref[...], winb_ref[...], gkn_ref[...],
                            gqn_ref[...], gmq_ref[...], bd_ref[...], cos, sin, SAMPLE_TERMS)
    k_ref[...] = k
    v_ref[...] = v
    q_ref[...] = q
    qn_ref[...] = qn


def _kvq_sample(pos, h2, p):
    B = h2.shape[0]
    return pl.pallas_call(
        _kvq_sample_kernel,
        grid=(1,),
        in_specs=[
            _full((B, 1)), _full((B, D_MODEL)), _full((1, D_MODEL)), _full((1, D_MODEL)),
            _full((D_MODEL, 2 * TOK_W)), _full((D_MODEL, D_MODEL)), _full((1, TOK_W)), _full((1, TOK_W)),
            _full((1, MEM_W)), _full((HEAD_BD, HEAD_BD)), _full((1, LANES)), _full((1, LANES)),
        ],
        out_specs=[_full((B, TOK_W)), _full((B, TOK_W)), _full((B, TOK_W)), _full((B, MEM_W))],
        out_shape=[_sds((B, TOK_W)), _sds((B, TOK_W)), _sds((B, TOK_W)), _sds((B, MEM_W))],
        compiler_params=_cparams(1),
        name="kvq_sample",
    )(pos, h2, p["g_kv"], p["g_mix"][1], p["w_kv"], p["w_in_b"], p["g_kn"], p["g_qn"], p["g_mem_qn"][1],
      p["bd"], p["rope_inv"], p["rope_sgn"])


def _top3_rows(sb, n_valid):
    lane = lax.broadcasted_iota(I32, sb.shape, 1).astype(F32)
    big = float(sb.shape[1])
    s = jnp.where(lane < n_valid, sb, -jnp.inf)
    picks = []
    for _ in range(MOBA_TOPK):
        mx = jnp.max(s, axis=-1, keepdims=True)
        idx = jnp.min(jnp.where(s == mx, lane, big), axis=-1, keepdims=True)
        picks.append(jnp.where(mx > -jnp.inf, idx, -1.0))
        s = jnp.where(lane == idx, -jnp.inf, s)
    return picks


def _moba_prompt_kernel(qb_ref, k_ref, v_ref, mean_ref, o_ref):
    b = pl.program_id(1)
    lane = lax.broadcasted_iota(I32, (1, LANES), 1)
    qb = qb_ref[...]
    means = mean_ref[...].astype(BF16)
    row = lax.broadcasted_iota(I32, (TM, MOBA_BLOCK), 0)
    col = lax.broadcasted_iota(I32, (TM, MOBA_BLOCK), 1)
    k_own = k_ref[b]
    v_own = v_ref[b]
    bf = b.astype(F32)

    qs, picks, state = [], [], []
    for hh in range(2):
        in_head = (lane // HEAD_DIM) == hh
        q_h = jnp.where(in_head, qb, jnp.zeros_like(qb))
        qs.append(q_h)
        picks.append(_top3_rows(_dot_nt(q_h, means) * (HEAD_DIM ** 0.5), bf))
        s = jnp.where(col <= row, _dot(q_h, k_own), NEG)
        m = jnp.max(s, axis=-1, keepdims=True)
        e = jnp.exp(s - m)
        state += [m, jnp.sum(e, axis=-1, keepdims=True), _dot_nt(e.astype(BF16), v_own)]

    def body(n, carry):
        k_n = k_ref[n]
        v_n = v_ref[n]
        nf = n.astype(F32)
        out = []
        for hh in range(2):
            m, l, acc = carry[3 * hh:3 * hh + 3]
            p1, p2, p3 = picks[hh]
            sel = (p1 == nf) | (p2 == nf) | (p3 == nf)
            s = jnp.where(sel, _dot(qs[hh], k_n), NEG)
            m_new = jnp.maximum(m, jnp.max(s, axis=-1, keepdims=True))
            alpha = jnp.exp(m - m_new)
            e = jnp.exp(s - m_new)
            out += [m_new, alpha * l + jnp.sum(e, axis=-1, keepdims=True), alpha * acc + _dot_nt(e.astype(BF16), v_n)]
        return tuple(out)

    m0, l0, a0, m1, l1, a1 = lax.fori_loop(0, b, body, tuple(state))
    o_ref[...] = jnp.where((lane // HEAD_DIM) == 0, a0 / l0, a1 / l1)


def _moba_prompt(qb, ktb, vtb, means):
    T = qb.shape[0]
    nbp = means.shape[0]
    qspec = pl.BlockSpec((TM, LANES), lambda hp, b: (b, hp))
    kvspec = pl.BlockSpec((T // MOBA_BLOCK, LANES, MOBA_BLOCK), lambda hp, b: (0, hp, 0))
    return pl.pallas_call(
        _moba_prompt_kernel,
        grid=(TOK_W // LANES, T // TM),
        in_specs=[qspec, kvspec, kvspec, pl.BlockSpec((nbp, LANES), lambda hp, b: (0, hp))],
        out_specs=qspec,
        out_shape=_sds((T, TOK_W)),
        compiler_params=_cparams(2),
        name="moba_prompt",
    )(qb, ktb, vtb, means)


SEL_ROWS = 16


def _moba_select_kernel(n_blocks, pt_ref, *refs):
    pages = refs[:MEANS_PAGES]
    q_ref, idx_ref, sc_ref = refs[MEANS_PAGES:]
    g = pl.program_id(1)
    lane = lax.broadcasted_iota(I32, (1, LANES), 1)
    blocks_per_step = MEANS_PAGES // PAGES_PER_BLOCK

    @pl.when(g == 0)
    def _():
        sc_ref[...] = jnp.zeros(sc_ref.shape, F32)

    q = _bf_round(q_ref[0])
    for jb in range(blocks_per_step):
        mean = jnp.sum(pages[2 * jb][0] + pages[2 * jb + 1][0], axis=-1, keepdims=True) * (1.0 / MOBA_BLOCK)
        prod = _bf_round(mean) * q
        n = g * blocks_per_step + jb
        for h in range(MOBA_HEADS):
            score = jnp.sum(prod[h * HEAD_DIM:(h + 1) * HEAD_DIM], axis=0, keepdims=True)
            sc_ref[h:h + 1, :] = jnp.where(lane == n, score, sc_ref[h:h + 1, :])

    @pl.when(g == pl.num_programs(1) - 1)
    def _():
        picks = _top3_rows(sc_ref[...], float(n_blocks))
        idx_ref[0] = jnp.where(lane == 0, picks[0], jnp.where(lane == 1, picks[1], picks[2])).astype(I32)


def _moba_select(q, pool_kt, page_table):
    B = q.shape[0]
    n_pages = page_table.shape[1]
    assert n_pages // PAGES_PER_BLOCK <= LANES

    def page_spec(jp):
        return pl.BlockSpec((1, TOK_W, PAGE_SIZE), lambda b, g, pt: (pt[b * n_pages + g * MEANS_PAGES + jp], 0, 0))

    grid_spec = pltpu.PrefetchScalarGridSpec(
        num_scalar_prefetch=1,
        grid=(B, n_pages // MEANS_PAGES),
        in_specs=[page_spec(jp) for jp in range(MEANS_PAGES)]
        + [pl.BlockSpec((1, TOK_W, 1), lambda b, g, pt: (b, 0, 0))],
        out_specs=pl.BlockSpec((1, SEL_ROWS, LANES), lambda b, g, pt: (b, 0, 0)),
        scratch_shapes=[pltpu.VMEM((SEL_ROWS, LANES), F32)],
    )
    return pl.pallas_call(
        functools.partial(_moba_select_kernel, n_pages // PAGES_PER_BLOCK),
        grid_spec=grid_spec,
        out_shape=_sds((B, SEL_ROWS, LANES), I32),
        compiler_params=_cparams(2),
        name="moba_select",
    )(page_table.reshape(-1), *([pool_kt] * MEANS_PAGES), q.reshape(B, TOK_W, 1))


def _moba_sample_kernel(ph_ref, *refs):
    kp = refs[:N_SEL_PAGES]
    vp = refs[N_SEL_PAGES:2 * N_SEL_PAGES]
    q_ref, kn_ref, vn_ref, o_ref = refs[2 * N_SEL_PAGES:]
    q = q_ref[0, 0] * (HEAD_DIM ** -0.5)
    q8 = jnp.broadcast_to(q, (8, HEAD_DIM))
    s = jnp.concatenate([_mm_bf(q8, r[0].astype(BF16)) for r in kp], axis=-1)
    s_own = jnp.sum(q * kn_ref[0, 0], axis=-1, keepdims=True)
    m = jnp.maximum(jnp.max(s, axis=-1, keepdims=True), s_own)
    e = jnp.exp(s - m)
    e_own = jnp.exp(s_own - m)
    l = jnp.sum(e, axis=-1, keepdims=True) + e_own
    p = (e / l).astype(BF16)
    o = (e_own / l) * vn_ref[0, 0]
    for jp, r in enumerate(vp):
        o = o + _dot_nt(p[:, jp * PAGE_SIZE:(jp + 1) * PAGE_SIZE], r[0].astype(BF16))
    o_ref[0, 0] = o[0:1]


def _moba_sample(q, k_new, v_new, pool_kt, pool_vt, phys):
    B = q.shape[0]

    def page_spec(jp):
        return pl.BlockSpec((1, HEAD_DIM, PAGE_SIZE),
                            lambda b, h, ph: (ph[(b * MOBA_HEADS + h) * N_SEL_PAGES + jp], h, 0))

    pages = [page_spec(jp) for jp in range(N_SEL_PAGES)]
    row = pl.BlockSpec((1, 1, 1, HEAD_DIM), lambda b, h, ph: (b, h, 0, 0))
    grid_spec = pltpu.PrefetchScalarGridSpec(
        num_scalar_prefetch=1,
        grid=(B, MOBA_HEADS),
        in_specs=pages + pages + [row, row, row],
        out_specs=row,
    )
    r4 = lambda a: a.reshape(B, MOBA_HEADS, 1, HEAD_DIM)
    out = pl.pallas_call(
        _moba_sample_kernel,
        grid_spec=grid_spec,
        out_shape=_sds((B, MOBA_HEADS, 1, HEAD_DIM)),
        compiler_params=_cparams(2),
        name="moba_sample",
    )(phys.reshape(-1), *([pool_kt] * N_SEL_PAGES), *([pool_vt] * N_SEL_PAGES), r4(q), r4(k_new), r4(v_new))
    return out.reshape(B, TOK_W)


def _combine_kernel(h_ref, y2_ref, gate_ref, o_ref):
    o_ref[...] = _moe_combine(h_ref[...], y2_ref[...], gate_ref[...])


def _combine(h, y2, gate):
    T = h.shape[0]
    tile = lambda w: pl.BlockSpec((TM, w), lambda i: (i, 0))
    return pl.pallas_call(
        _combine_kernel,
        grid=(T // TM,),
        in_specs=[tile(D_MODEL), tile(EXP_TOPK * D_MODEL), tile(LANES)],
        out_specs=tile(D_MODEL),
        out_shape=_sds((T, D_MODEL)),
        compiler_params=_cparams(1),
        name="moe_combine",
    )(h, y2, gate)


def _bd_const():
    i = np.arange(HEAD_BD) // HEAD_DIM
    return jnp.asarray(i[:, None] == i[None, :], dtype=BF16)


def _rope_consts():
    lane = np.arange(LANES) % HEAD_DIM
    half = ROT_DIM // 2
    inv = 1.0 / (ROPE_THETA ** (np.arange(half, dtype=np.float32) * 2.0 / ROT_DIM))
    inv_l = np.where(lane < ROT_DIM, inv[lane % half], 0.0).astype(np.float32)
    sgn = np.where(lane < half, -1.0, np.where(lane < ROT_DIM, 1.0, 0.0)).astype(np.float32)
    return jnp.asarray(inv_l)[None], jnp.asarray(sgn)[None]


def _feature_major(a, n_lead):
    lead = a.shape[:n_lead]
    rows, heads, dim = a.shape[n_lead:]
    perm = tuple(range(n_lead)) + (n_lead + 1, n_lead + 2, n_lead)
    return jnp.transpose(a, perm).reshape(*lead, heads * dim, rows)


def _token_major(a, n_lead, heads):
    lead = a.shape[:n_lead]
    feat, rows = a.shape[n_lead:]
    perm = tuple(range(n_lead)) + (n_lead + 2, n_lead, n_lead + 1)
    return jnp.transpose(a.reshape(*lead, heads, feat // heads, rows), perm)


def _prep_params(g_mix, g_ffn, w_out, g_mem_qn, w_in_a, g_sgu, b_sgu, w_spatial, b_spatial, g_kv, w_kv, g_kn, w_in_b,
                 g_qn, w_r1, b_r1, w_r2, b_r2):
    depth = g_mix.shape[0]
    assert w_in_a.shape[0] == 1 and w_in_b.shape[0] == 1, "one mixer-A layer followed by one mixer-B layer"
    row = lambda a: a.reshape(a.shape[0], 1, a.shape[-1])
    wr = jnp.concatenate([w_r1, w_r2, jnp.zeros((depth, D_MODEL, LANES - N_GROUPS - N_EXPERTS), F32)], axis=-1)
    inv, sgn = _rope_consts()
    return {
        "g_mix": row(g_mix), "g_ffn": row(g_ffn),
        "w_out_t": w_out[:, :TOK_W].astype(BF16), "w_out_m": w_out[:, TOK_W:].astype(BF16),
        "g_mem_qn": row(jnp.tile(g_mem_qn, (1, MEM_HEADS))),
        "w_in_a": w_in_a[0].astype(BF16),
        "g_sgu": g_sgu, "b_sgu": b_sgu, "w_spatial": w_spatial[0],
        "b_spatial_full": jnp.broadcast_to(b_spatial[0][:, :, None], (SGU_GROUPS, CHUNK, SGU_GROUP_W)),
        "sgu_w00": jnp.repeat(w_spatial[0][:, 0, 0], SGU_GROUP_W)[None],
        "sgu_b0": jnp.repeat(b_spatial[0][:, 0], SGU_GROUP_W)[None],
        "g_kv": g_kv[None], "w_kv": w_kv.astype(BF16), "g_kn": jnp.tile(g_kn, MOBA_HEADS)[None],
        "w_in_b": w_in_b[0].astype(BF16), "g_qn": jnp.tile(g_qn[0], MOBA_HEADS)[None],
        "wr": wr.astype(BF16),
        "br": row(jnp.concatenate([b_r1, b_r2, jnp.zeros((depth, LANES - N_GROUPS - N_EXPERTS), F32)], axis=-1)),
        "bd": _bd_const(), "rope_inv": inv, "rope_sgn": sgn,
    }


def kernel(x_prompt, x_sample, cache_k_pool, cache_v_pool, cache_mem_k, cache_mem_v, page_table, mem_prompt,
           g_mix, g_ffn, w_out, g_mem, w_mem_kv, g_mem_qn, g_mem_kn, w_in_a, g_sgu, b_sgu, w_spatial, b_spatial,
           g_kv, w_kv, g_kn, w_in_b, g_qn, w_r1, b_r1, w_r2, b_r2, w_e_gate, w_e_up, w_e_down):
    n_batch, T, _ = x_prompt.shape
    B, dec_seq, _ = x_sample.shape
    depth = g_mix.shape[0]
    n_pages = page_table.shape[1]
    past_len = n_pages * PAGE_SIZE
    assert n_batch == 1 and dec_seq == 1 and depth == 2
    assert T % TM == 0 and B % 8 == 0
    assert past_len % MOBA_BLOCK == 0 and past_len // MOBA_BLOCK >= MOBA_TOPK and n_pages % MEANS_PAGES == 0

    p = _prep_params(g_mix, g_ffn, w_out, g_mem_qn, w_in_a, g_sgu, b_sgu, w_spatial, b_spatial, g_kv, w_kv, g_kn,
                     w_in_b, g_qn, w_r1, b_r1, w_r2, b_r2)
    bd = p["bd"]

    mem_kt_p, mem_vt_p, p["mkbd"], p["mvbd"] = _mem_kv(
        mem_prompt[0], g_mem, w_mem_kv.astype(BF16), jnp.tile(g_mem_kn, (1, MEM_HEADS)), bd)
    mem_kt_s = _feature_major(cache_mem_k, 2)
    mem_vt_s = _feature_major(cache_mem_v, 2)

    h1, vn_p, xn2, eid, gate = _mix_a_prompt(x_prompt[0], p)
    y2 = _moe_experts(xn2, eid[:, :EXP_TOPK], w_e_gate[0], w_e_up[0], w_e_down[0])
    h2, kt_p, vt_p, ktb, vtb, qb, means, mo1 = _kvq_prompt(h1, y2, gate, p)
    nblk = T // TM
    nbp = -(-nblk // LANES) * LANES
    means_p = jnp.pad(means.reshape(nblk, TOK_W), ((0, nbp - nblk), (0, 0)))
    tok1 = _moba_prompt(qb, ktb, vtb, means_p)
    h3, xn2b, eid_b, gate_b = _out_router_prompt(h2, tok1, mo1, p, 1)
    y2b = _moe_experts(xn2b, eid_b[:, :EXP_TOPK], w_e_gate[1], w_e_up[1], w_e_down[1])
    y_p = _combine(h3, y2b, gate_b)

    xs = x_sample[:, 0]
    tok_s, vn_s, qn_s = _mix_a_sample(xs, p)
    mo_s = _mem_attend_sample(qn_s, mem_kt_s, mem_vt_s, 0)
    h1_s, xn2_s, gd_s = _out_router_sample(xs, tok_s, mo_s, p, 0)
    h2_s = _moe_sample(h1_s, xn2_s, gd_s, w_e_gate[0], w_e_up[0], w_e_down[0])
    pos_s = jnp.full((B, 1), float(past_len), F32)
    k_s, v_s, q_s, qn1_s = _kvq_sample(pos_s, h2_s, p)
    pool_kt = _feature_major(cache_k_pool, 1)
    pool_vt = _feature_major(cache_v_pool, 1)
    sel = _moba_select(q_s, pool_kt, page_table)[:, :MOBA_HEADS, :MOBA_TOPK]
    lp = sel[..., None] * PAGES_PER_BLOCK + jnp.arange(PAGES_PER_BLOCK, dtype=I32)
    phys = jnp.take_along_axis(page_table[:, None, :], lp.reshape(B, 1, -1), axis=2)
    tok1_s = _moba_sample(q_s, k_s, v_s, pool_kt, pool_vt, phys.reshape(B, MOBA_HEADS, N_SEL_PAGES))
    mo1_s = _mem_attend_sample(qn1_s, mem_kt_s, mem_vt_s, 1)
    h3_s, xn2b_s, gd_b = _out_router_sample(h2_s, tok1_s, mo1_s, p, 1)
    y_s = _moe_sample(h3_s, xn2b_s, gd_b, w_e_gate[1], w_e_up[1], w_e_down[1])

    hd = (MOBA_HEADS, HEAD_DIM)
    return (
        y_p[None], y_s[:, None],
        _token_major(kt_p[None], 1, MOBA_HEADS), _token_major(vt_p[None], 1, MOBA_HEADS),
        k_s.reshape(B, 1, *hd), v_s.reshape(B, 1, *hd),
        _token_major(mem_kt_p[:, None], 2, MEM_HEADS), _token_major(mem_vt_p[:, None], 2, MEM_HEADS),
        vn_p[None, None], vn_s[None, :, None],
    )
```
